```python
import jax, jax.numpy as jnp
from jax import lax
import numpy as np

D_MODEL = 1024
BATCH = 2
SEQ = 8192
DEPTH = 2
DEC_BATCH = 32
DEC_SEQ = 8
PAST_LEN = 16384
PAGE_SIZE = 128

N_MIXERS = 2
N_A = (DEPTH + 1) // 2
N_B = DEPTH // 2
D_FF = 2816
D_PLE = 256
EPS = 1e-6
ROPE_THETA = 500000.0
ROT_DIV = 4
Q_BLOCK = 128
NEG_INF = -1e30

HA = 8
HD_A = 64
VD_A = 2 * HD_A
KVH_A = 4
G_A = HA // KVH_A
A_Q_W = HA * 2 * HD_A
A_K_W = KVH_A * 2 * HD_A
A_V_W = KVH_A * VD_A
A_IN_W = A_Q_W + A_K_W + A_V_W

HB = 16
HD_B = 64
KVH_B = 4
G_B = HB // KVH_B
IDX_H = 8
IDX_D = 64
TOPK_MAX = 256
B_Q_W = HB * HD_B
B_K_W = KVH_B * HD_B
B_V_W = KVH_B * HD_B
B_QI_W = IDX_H * IDX_D
B_KI_W = IDX_D
B_WI_W = IDX_H
B_IN_W = B_Q_W + B_K_W + B_V_W + B_QI_W + B_KI_W + B_WI_W

kernel_name = 'hybrid_diffattn_dsa_macaron_step'


def rmsnorm(x, g):
    xf = x.astype(jnp.float32)
    y = xf * lax.rsqrt(jnp.mean(xf * xf, axis=-1, keepdims=True) + EPS)
    return (y * g.astype(jnp.float32)).astype(x.dtype)


def swiglu(h, w_in, w_out):
    gate, up = jnp.split(h @ w_in, 2, axis=-1)
    return (jax.nn.silu(gate) * up) @ w_out


def apply_partial_rope(x, pos):
    hd = x.shape[-1]
    r = hd // ROT_DIV
    half = r // 2
    inv = jnp.power(jnp.float32(ROPE_THETA), -jnp.arange(half, dtype=jnp.float32) * (2.0 / r))
    ang = pos.astype(jnp.float32)[:, None] * inv[None, :]
    shp = (pos.shape[0],) + (1,) * (x.ndim - 3) + (half,)
    c = jnp.cos(ang).reshape(shp).astype(x.dtype)
    s = jnp.sin(ang).reshape(shp).astype(x.dtype)
    x1, x2, rest = x[..., :half], x[..., half:r], x[..., r:]
    return jnp.concatenate([x1 * c - x2 * s, x2 * c + x1 * s, rest], axis=-1)


def to_blocks(a, n_blocks):
    return jnp.moveaxis(a.reshape((a.shape[0], n_blocks, Q_BLOCK) + a.shape[2:]), 1, 0)


def from_blocks(a):
    a = jnp.moveaxis(a, 0, 1)
    return a.reshape((a.shape[0], a.shape[1] * a.shape[2]) + a.shape[3:])


def diff_lambda(lq1, lk1, lq2, lk2, lam_init):
    f = jnp.float32
    return (jnp.exp(jnp.sum(lq1.astype(f) * lk1.astype(f)))
            - jnp.exp(jnp.sum(lq2.astype(f) * lk2.astype(f))) + lam_init)


def diff_proj(h, w_in, pos):
    B, T, _ = h.shape
    q, k, v = jnp.split(h @ w_in, [A_Q_W, A_Q_W + A_K_W], axis=-1)
    q = apply_partial_rope(q.reshape(B, T, KVH_A, G_A, 2, HD_A), pos)
    k = apply_partial_rope(k.reshape(B, T, KVH_A, 2, HD_A), pos)
    v = v.reshape(B, T, KVH_A, VD_A)
    return q, k, v


def diff_core(q, k, v, q_pos, k_pos, lam):
    s = jnp.einsum('btkgmd,bskmd->bkgmts', q, k).astype(jnp.float32) * (HD_A ** -0.5)
    s = jnp.where(k_pos[None, :] <= q_pos[:, None], s, NEG_INF)
    a = jax.nn.softmax(s, axis=-1)
    a = a[:, :, :, 0] - lam * a[:, :, :, 1]
    return jnp.einsum('bkgts,bskv->btkgv', a.astype(v.dtype), v)


def diff_out(o, g_sub, lam_init, w_out):
    B, T = o.shape[:2]
    o = rmsnorm(o, g_sub) * (1.0 - lam_init)
    return o.reshape(B, T, HA * VD_A) @ w_out


def diff_attn_prompt(h, w_in, w_out, g_sub, lam, lam_init):
    B, T, _ = h.shape
    pos = jnp.arange(T)
    q, k, v = diff_proj(h, w_in, pos)
    nb = T // Q_BLOCK
    ob = lax.map(lambda a: diff_core(a[0], k, v, a[1], pos, lam),
                 (to_blocks(q, nb), pos.reshape(nb, Q_BLOCK)))
    return diff_out(from_blocks(ob), g_sub, lam_init, w_out), k, v


def diff_attn_sample(h, cache_k, cache_v, j, page_table, w_in, w_out, g_sub, lam, lam_init):
    DB, T, _ = h.shape
    past = page_table.shape[1] * PAGE_SIZE
    pos = past + jnp.arange(T)
    q, k, v = diff_proj(h, w_in, pos)
    k_past = cache_k[j, page_table].reshape((DB, past) + k.shape[2:]).astype(k.dtype)
    v_past = cache_v[j, page_table].reshape((DB, past) + v.shape[2:]).astype(v.dtype)
    k_all = jnp.concatenate([k_past, k], axis=1)
    v_all = jnp.concatenate([v_past, v], axis=1)
    o = diff_core(q, k_all, v_all, pos, jnp.arange(past + T), lam)
    return diff_out(o, g_sub, lam_init, w_out), k, v


def dsa_proj(h, w_in, pos):
    B, T, _ = h.shape
    c = np.cumsum([B_Q_W, B_K_W, B_V_W, B_QI_W, B_KI_W]).tolist()
    q, k, v, qi, ki, wi = jnp.split(h @ w_in, c, axis=-1)
    q = apply_partial_rope(q.reshape(B, T, KVH_B, G_B, HD_B), pos)
    k = apply_partial_rope(k.reshape(B, T, KVH_B, HD_B), pos)
    v = v.reshape(B, T, KVH_B, HD_B)
    qi = apply_partial_rope(qi.reshape(B, T, IDX_H, IDX_D), pos)
    ki = apply_partial_rope(ki, pos)
    wi = wi * ((IDX_H ** -0.5) * (IDX_D ** -0.5))
    return q, k, v, qi, ki, wi


def indexer_topk(qi, wi, ki_all, q_pos, k_pos, n_sel):
    sc = jnp.einsum('bthd,bsd->bths', qi, ki_all).astype(jnp.float32)
    score = jnp.einsum('bths,bth->bts', jax.nn.relu(sc), wi.astype(jnp.float32))
    score = jnp.where(k_pos[None, :] <= q_pos[:, None], score, NEG_INF)
    _, idx = lax.top_k(score, n_sel)
    return idx


def sparse_core(q, k_sel, v_sel, valid):
    s = jnp.einsum('btkgd,btnkd->btkgn', q, k_sel).astype(jnp.float32) * (HD_B ** -0.5)
    s = jnp.where(valid[:, :, None, None, :], s, NEG_INF)
    a = jax.nn.softmax(s, axis=-1)
    return jnp.einsum('btkgn,btnkd->btkgd', a.astype(v_sel.dtype), v_sel)


def dsa_prompt(h, w_in, w_out):
    B, T, _ = h.shape
    pos = jnp.arange(T)
    q, k, v, qi, ki, wi = dsa_proj(h, w_in, pos)
    n_sel = min(TOPK_MAX, T // 4)
    nb = T // Q_BLOCK
    bi = jnp.arange(B)[:, None, None]

    def one_block(a):
        qb, qib, wib, pb = a
        idx = indexer_topk(qib, wib, ki, pb, pos, n_sel)
        return sparse_core(qb, k[bi, idx], v[bi, idx], idx <= pb[None, :, None])

    ob = lax.map(one_block, (to_blocks(q, nb), to_blocks(qi, nb), to_blocks(wi, nb),
                             pos.reshape(nb, Q_BLOCK)))
    o = from_blocks(ob).reshape(B, T, HB * HD_B)
    return o @ w_out, k, v, ki


def dsa_sample(h, cache_k, cache_v, cache_kidx, j, page_table, w_in, w_out):
    DB, T, _ = h.shape
    past = page_table.shape[1] * PAGE_SIZE
    pos = past + jnp.arange(T)
    q, k, v, qi, ki, wi = dsa_proj(h, w_in, pos)
    ki_past = cache_kidx[j, page_table].reshape(DB, past, IDX_D).astype(ki.dtype)
    ki_all = jnp.concatenate([ki_past, ki], axis=1)
    n_sel = min(TOPK_MAX, (past + T) // 4)
    idx = indexer_topk(qi, wi, ki_all, pos, jnp.arange(past + T), n_sel)
    bi = jnp.arange(DB)[:, None, None]
    idx_p = jnp.minimum(idx, past - 1)
    phys = page_table[bi, idx_p // PAGE_SIZE]
    off = idx_p % PAGE_SIZE
    idx_n = jnp.clip(idx - past, 0, T - 1)
    is_new = (idx >= past)[..., None, None]
    k_sel = jnp.where(is_new, k[bi, idx_n], cache_k[j, phys, off].astype(k.dtype))
    v_sel = jnp.where(is_new, v[bi, idx_n], cache_v[j, phys, off].astype(v.dtype))
    o = sparse_core(q, k_sel, v_sel, idx <= pos[None, :, None])
    return o.reshape(DB, T, HB * HD_B) @ w_out, k, v, ki


def half_ffn(h, g, w_in, w_out):
    return h + 0.5 * swiglu(rmsnorm(h, g), w_in, w_out)


def per_layer_embed(h, p_i, g, w_gate, w_proj):
    return h + jax.nn.sigmoid(rmsnorm(h, g) @ w_gate) * (p_i @ w_proj)


def setup_inputs(seed: int = 0) -> dict:
    key = jax.random.key(seed)
    ks = jax.random.split(key, 32)
    f32 = jnp.float32

    def nrm(k, shape, scale=1.0):
        return jax.random.normal(k, shape, f32) * scale

    n_pages = PAST_LEN // PAGE_SIZE
    n_used = DEC_BATCH * n_pages
    n_pool = n_used + max(1, n_used // 4)
    page_table = jax.random.permutation(ks[0], n_pool)[:n_used].reshape(DEC_BATCH, n_pages).astype(jnp.int32)
    return {
        'x_prompt': nrm(ks[1], (BATCH, SEQ, D_MODEL)),
        'x_sample': nrm(ks[2], (DEC_BATCH, DEC_SEQ, D_MODEL)),
        'cache_a_k': nrm(ks[3], (N_A, n_pool, PAGE_SIZE, KVH_A, 2, HD_A)),
        'cache_a_v': nrm(ks[4], (N_A, n_pool, PAGE_SIZE, KVH_A, VD_A)),
        'cache_b_k': nrm(ks[5], (N_B, n_pool, PAGE_SIZE, KVH_B, HD_B)),
        'cache_b_v': nrm(ks[6], (N_B, n_pool, PAGE_SIZE, KVH_B, HD_B)),
        'cache_b_kidx': nrm(ks[7], (N_B, n_pool, PAGE_SIZE, IDX_D)),
        'page_table': page_table,
        'p_prompt': nrm(ks[8], (DEPTH, BATCH, SEQ, D_PLE)),
        'p_sample': nrm(ks[9], (DEPTH, DEC_BATCH, DEC_SEQ, D_PLE)),
        'norm_ffn1': 1.0 + nrm(ks[10], (DEPTH, D_MODEL), 0.02),
        'w_ffn1_in': nrm(ks[11], (DEPTH, D_MODEL, 2 * D_FF), D_MODEL ** -0.5),
        'w_ffn1_out': nrm(ks[12], (DEPTH, D_FF, D_MODEL), D_FF ** -0.5),
        'norm_mix': 1.0 + nrm(ks[13], (DEPTH, D_MODEL), 0.02),
        'norm_ffn2': 1.0 + nrm(ks[14], (DEPTH, D_MODEL), 0.02),
        'w_ffn2_in': nrm(ks[15], (DEPTH, D_MODEL, 2 * D_FF), D_MODEL ** -0.5),
        'w_ffn2_out': nrm(ks[16], (DEPTH, D_FF, D_MODEL), D_FF ** -0.5),
        'norm_ple': 1.0 + nrm(ks[17], (DEPTH, D_MODEL), 0.02),
        'w_ple_gate': nrm(ks[18], (DEPTH, D_MODEL, D_MODEL), D_MODEL ** -0.5),
        'w_ple_proj': nrm(ks[19], (DEPTH, D_PLE, D_MODEL), D_PLE ** -0.5),
        'w_a_in': nrm(ks[20], (N_A, D_MODEL, A_IN_W), D_MODEL ** -0.5),
        'w_a_out': nrm(ks[21], (N_A, HA * VD_A, D_MODEL), (HA * VD_A) ** -0.5),
        'lambda_q1': nrm(ks[22], (N_A, HD_A), 0.1),
        'lambda_k1': nrm(ks[23], (N_A, HD_A), 0.1),
        'lambda_q2': nrm(ks[24], (N_A, HD_A), 0.1),
        'lambda_k2': nrm(ks[25], (N_A, HD_A), 0.1),
        'subln_a': 1.0 + nrm(ks[26], (N_A, VD_A), 0.02),
        'w_b_in': nrm(ks[27], (N_B, D_MODEL, B_IN_W), D_MODEL ** -0.5),
        'w_b_out': nrm(ks[28], (N_B, HB * HD_B, D_MODEL), (HB * HD_B) ** -0.5),
        'norm_final': 1.0 + nrm(ks[29], (D_MODEL,), 0.02),
    }


def reference(x_prompt, x_sample, cache_a_k, cache_a_v, cache_b_k, cache_b_v, cache_b_kidx,
              page_table, p_prompt, p_sample,
              norm_ffn1, w_ffn1_in, w_ffn1_out, norm_mix, norm_ffn2, w_ffn2_in, w_ffn2_out,
              norm_ple, w_ple_gate, w_ple_proj,
              w_a_in, w_a_out, lambda_q1, lambda_k1, lambda_q2, lambda_k2, subln_a,
              w_b_in, w_b_out, norm_final):
    hp, hs = x_prompt, x_sample
    a_kp, a_vp, a_ks, a_vs = [], [], [], []
    b_kp, b_vp, b_ip, b_ks, b_vs, b_is = [], [], [], [], [], []
    for i in range(DEPTH):
        j = i // N_MIXERS
        hp = half_ffn(hp, norm_ffn1[i], w_ffn1_in[i], w_ffn1_out[i])
        hs = half_ffn(hs, norm_ffn1[i], w_ffn1_in[i], w_ffn1_out[i])
        np_ = rmsnorm(hp, norm_mix[i])
        ns_ = rmsnorm(hs, norm_mix[i])
        if i % N_MIXERS == 0:
            lam_init = 0.8 - 0.6 * float(np.exp(-0.3 * i))
            lam = diff_lambda(lambda_q1[j], lambda_k1[j], lambda_q2[j], lambda_k2[j], lam_init)
            mp, kp, vp = diff_attn_prompt(np_, w_a_in[j], w_a_out[j], subln_a[j], lam, lam_init)
            ms, ks_, vs_ = diff_attn_sample(ns_, cache_a_k, cache_a_v, j, page_table,
                                            w_a_in[j], w_a_out[j], subln_a[j], lam, lam_init)
            a_kp.append(kp); a_vp.append(vp); a_ks.append(ks_); a_vs.append(vs_)
        else:
            mp, kp, vp, ip = dsa_prompt(np_, w_b_in[j], w_b_out[j])
            ms, ks_, vs_, is_ = dsa_sample(ns_, cache_b_k, cache_b_v, cache_b_kidx, j, page_table,
                                           w_b_in[j], w_b_out[j])
            b_kp.append(kp); b_vp.append(vp); b_ip.append(ip)
            b_ks.append(ks_); b_vs.append(vs_); b_is.append(is_)
        hp = hp + mp
        hs = hs + ms
        hp = half_ffn(hp, norm_ffn2[i], w_ffn2_in[i], w_ffn2_out[i])
        hs = half_ffn(hs, norm_ffn2[i], w_ffn2_in[i], w_ffn2_out[i])
        hp = per_layer_embed(hp, p_prompt[i], norm_ple[i], w_ple_gate[i], w_ple_proj[i])
        hs = per_layer_embed(hs, p_sample[i], norm_ple[i], w_ple_gate[i], w_ple_proj[i])
    y_prompt = rmsnorm(hp, norm_final)
    y_sample = rmsnorm(hs, norm_final)
    return (y_prompt, y_sample,
            jnp.stack(a_kp), jnp.stack(a_vp), jnp.stack(b_kp), jnp.stack(b_vp), jnp.stack(b_ip),
            jnp.stack(a_ks), jnp.stack(a_vs), jnp.stack(b_ks), jnp.stack(b_vs), jnp.stack(b_is))
```

```python
import functools

import numpy as np
import jax
import jax.numpy as jnp
from jax import lax
from jax.experimental import pallas as pl
from jax.experimental.pallas import tpu as pltpu

F32 = jnp.float32
BF16 = jnp.bfloat16
I32 = jnp.int32

N_MIXERS = 2
D_FF = 2816
EPS = 1e-6
ROPE_THETA = 500000.0
ROT_DIM = 16
HEAD_DIM = 64
PAGE_SIZE = 128
NEG_INF = -1e30
KVH_A, G_A, VD_A = 4, 2, 128
KVH_B, G_B = 4, 4
IDX_H = 8
TOPK_MAX = 256
A_Q_W, A_K_W = 1024, 512
B_Q_W, B_K_W, B_V_W, B_QI_W = 1024, 256, 256, 512

LANES = 128
VMEM_LIMIT_BYTES = 56 * 1024 * 1024
ROW_TILE = 512
ATTN_BLOCK = 256
PAGES_PER_STEP = 8
INT_MIN = -(2 ** 31)


def _cparams(sem):
    return pltpu.CompilerParams(dimension_semantics=sem, vmem_limit_bytes=VMEM_LIMIT_BYTES)


def _rms(x, g):
    ms = jnp.mean(x * x, axis=-1, keepdims=True)
    return x * lax.rsqrt(ms + EPS) * g


def _rope(x, c, sa, sb):
    outs = []
    for j in range(x.shape[1] // LANES):
        xc = x[:, j * LANES:(j + 1) * LANES]
        up = pltpu.roll(xc, LANES - ROT_DIM // 2, 1)
        dn = pltpu.roll(xc, ROT_DIM // 2, 1)
        outs.append(xc * c + up * sa + dn * sb)
    return outs[0] if len(outs) == 1 else jnp.concatenate(outs, axis=1)


def _rope_tables(pos):
    half = ROT_DIM // 2
    inv = jnp.power(jnp.float32(ROPE_THETA), -jnp.arange(half, dtype=F32) * (2.0 / ROT_DIM))
    ang = pos.astype(F32)[:, None] * inv[None, :]
    c, s = jnp.cos(ang), jnp.sin(ang)
    t = pos.shape[0]
    ones = jnp.ones((t, HEAD_DIM - ROT_DIM), F32)
    z = lambda w: jnp.zeros((t, w), F32)
    c64 = jnp.concatenate([c, c, ones], axis=1)
    sa64 = jnp.concatenate([-s, z(HEAD_DIM - half)], axis=1)
    sb64 = jnp.concatenate([z(half), s, z(HEAD_DIM - ROT_DIM)], axis=1)
    two = lambda a: jnp.concatenate([a, a], axis=1)
    return two(c64), two(sa64), two(sb64)


def _row_tile(n):
    return ROW_TILE if n % ROW_TILE == 0 else n


def _ffn_kernel(h_ref, g_ref, win_ref, wout_ref, o_ref, act_ref):
    h = h_ref[...]
    x = _rms(h, g_ref[...]).astype(BF16)
    chunk = 2 * LANES
    for c in range(D_FF // chunk):
        gate = jnp.dot(x, win_ref[:, c * chunk:(c + 1) * chunk], preferred_element_type=F32)
        up = jnp.dot(x, win_ref[:, D_FF + c * chunk:D_FF + (c + 1) * chunk], preferred_element_type=F32)
        act_ref[:, c * chunk:(c + 1) * chunk] = (gate * jax.nn.sigmoid(gate) * up).astype(BF16)
    y = jnp.dot(act_ref[...], wout_ref[...], preferred_element_type=F32)
    o_ref[...] = h + 0.5 * y


def _half_ffn(h, g, w_in, w_out):
    n, d = h.shape
    tm = _row_tile(n)
    const = lambda i: (0, 0)
    return pl.pallas_call(
        _ffn_kernel,
        grid=(n // tm,),
        in_specs=[pl.BlockSpec((tm, d), lambda i: (i, 0)),
                  pl.BlockSpec((1, d), const),
                  pl.BlockSpec(w_in.shape, const, pipeline_mode=pl.Buffered(1)),
                  pl.BlockSpec(w_out.shape, const, pipeline_mode=pl.Buffered(1))],
        out_specs=pl.BlockSpec((tm, d), lambda i: (i, 0)),
        out_shape=jax.ShapeDtypeStruct((n, d), F32),
        scratch_shapes=[pltpu.VMEM((tm, D_FF), BF16)],
        compiler_params=_cparams(("parallel",)),
        name="half_ffn",
    )(h, g.reshape(1, d), w_in, w_out)


def _ple_kernel(*refs, final):
    if final:
        h_ref, p_ref, g_ref, wg_ref, wp_ref, gf_ref, o_ref = refs
    else:
        h_ref, p_ref, g_ref, wg_ref, wp_ref, o_ref = refs
    h = h_ref[...]
    x = _rms(h, g_ref[...]).astype(BF16)
    gate = jax.nn.sigmoid(jnp.dot(x, wg_ref[...], preferred_element_type=F32))
    proj = jnp.dot(p_ref[...].astype(BF16), wp_ref[...], preferred_element_type=F32)
    hn = h + gate * proj
    o_ref[...] = _rms(hn, gf_ref[...]) if final else hn


def _ple(h, p, g, w_gate, w_proj, g_final=None):
    n, d = h.shape
    tm = _row_tile(n)
    const = lambda i: (0, 0)
    final = g_final is not None
    in_specs = [pl.BlockSpec((tm, d), lambda i: (i, 0)),
                pl.BlockSpec((tm, p.shape[1]), lambda i: (i, 0)),
                pl.BlockSpec((1, d), const),
                pl.BlockSpec(w_gate.shape, const),
                pl.BlockSpec(w_proj.shape, const)]
    args = [h, p, g.reshape(1, d), w_gate, w_proj]
    if final:
        in_specs.append(pl.BlockSpec((1, d), const))
        args.append(g_final.reshape(1, d))
    return pl.pallas_call(
        functools.partial(_ple_kernel, final=final),
        grid=(n // tm,),
        in_specs=in_specs,
        out_specs=pl.BlockSpec((tm, d), lambda i: (i, 0)),
        out_shape=jax.ShapeDtypeStruct((n, d), F32),
        compiler_params=_cparams(("parallel",)),
        name="ple",
    )(*args)


def _a_proj_kernel(h_ref, g_ref, w_ref, c_ref, sa_ref, sb_ref,
                   q_ref, k_ref, v_ref, kt_ref, vb_ref):
    x = _rms(h_ref[...], g_ref[...]).astype(BF16)
    qkv = jnp.dot(x, w_ref[...], preferred_element_type=F32)
    c, sa, sb = c_ref[...], sa_ref[...], sb_ref[...]
    q = _rope(qkv[:, :A_Q_W], c, sa, sb)
    k = _rope(qkv[:, A_Q_W:A_Q_W + A_K_W], c, sa, sb)
    v = qkv[:, A_Q_W + A_K_W:]
    q_ref[...] = (q * (HEAD_DIM ** -0.5)).astype(BF16)
    k_ref[...] = k
    v_ref[...] = v
    kt_ref[...] = k.T.astype(BF16)
    vb_ref[...] = v.astype(BF16)


def _a_proj(h, g, w, tables, nb):
    n, d = h.shape
    t = n // nb
    tm = _row_tile(t)
    nt = t // tm
    const = lambda i: (0, 0)
    row = lambda i: (i, 0)
    tab = lambda i: (i % nt, 0)
    kw, vw = A_K_W, w.shape[1] - A_Q_W - A_K_W
    return pl.pallas_call(
        _a_proj_kernel,
        grid=(n // tm,),
        in_specs=[pl.BlockSpec((tm, d), row), pl.BlockSpec((1, d), const), pl.BlockSpec(w.shape, const),
                  pl.BlockSpec((tm, LANES), tab), pl.BlockSpec((tm, LANES), tab), pl.BlockSpec((tm, LANES), tab)],
        out_specs=[pl.BlockSpec((tm, A_Q_W), row), pl.BlockSpec((tm, kw), row), pl.BlockSpec((tm, vw), row),
                   pl.BlockSpec((None, kw, tm), lambda i: (i // nt, 0, i % nt)),
                   pl.BlockSpec((tm, vw), row)],
        out_shape=[jax.ShapeDtypeStruct((n, A_Q_W), BF16), jax.ShapeDtypeStruct((n, kw), F32),
                   jax.ShapeDtypeStruct((n, vw), F32), jax.ShapeDtypeStruct((nb, kw, t), BF16),
                   jax.ShapeDtypeStruct((n, vw), BF16)],
        compiler_params=_cparams(("parallel",)),
        name="a_proj",
    )(h, g.reshape(1, d), w, *tables)


def _diff_lambda(lq1_ref, lk1_ref, lq2_ref, lk2_ref, lam_init):
    s1 = jnp.sum(lq1_ref[...] * lk1_ref[...], axis=1, keepdims=True)
    s2 = jnp.sum(lq2_ref[...] * lk2_ref[...], axis=1, keepdims=True)
    return jnp.exp(s1) - jnp.exp(s2) + lam_init


def _softmax_step(s, v, m_ref, l_ref, acc_ref):
    m_prev = m_ref[...]
    m_new = jnp.maximum(m_prev, jnp.max(s, axis=1, keepdims=True))
    alpha = jnp.exp(m_prev - m_new)
    p = jnp.exp(s - m_new)
    l_ref[...] = alpha * l_ref[...] + jnp.sum(p, axis=1, keepdims=True)
    acc_ref[...] = alpha * acc_ref[...] + jnp.dot(p.astype(BF16), v, preferred_element_type=F32)
    m_ref[...] = m_new


def _attn_a_kernel(lq1_ref, lk1_ref, lq2_ref, lk2_ref, q_ref, kt_ref, v_ref, o_ref,
                   m_ref, l_ref, acc_ref, *, blk, lam_init):
    i = pl.program_id(2)
    q = q_ref[0]
    lo = lax.broadcasted_iota(I32, (blk, LANES), 1) < HEAD_DIM
    zero = jnp.zeros((blk, LANES), BF16)
    qa, qb = q[:, :LANES], q[:, LANES:]
    qs = jnp.concatenate([jnp.where(lo, qa, zero), jnp.where(lo, zero, qa),
                          jnp.where(lo, qb, zero), jnp.where(lo, zero, qb)], axis=0)
    m_ref[...] = jnp.full(m_ref.shape, NEG_INF, F32)
    l_ref[...] = jnp.zeros(l_ref.shape, F32)
    acc_ref[...] = jnp.zeros(acc_ref.shape, F32)

    def step(j, masked):
        start = pl.multiple_of(j * blk, blk)
        s = jnp.dot(qs, kt_ref[0, :, pl.ds(start, blk)], preferred_element_type=F32)
        if masked:
            qpos = lax.broadcasted_iota(I32, (4, blk, blk), 1).reshape(4 * blk, blk)
            kpos = lax.broadcasted_iota(I32, (4 * blk, blk), 1)
            s = jnp.where(kpos <= qpos, s, NEG_INF)
        _softmax_step(s, v_ref[0, pl.ds(start, blk), :], m_ref, l_ref, acc_ref)

    def body(j, carry):
        step(j, False)
        return carry

    lax.fori_loop(0, i, body, 0)
    step(i, True)

    lam = _diff_lambda(lq1_ref, lk1_ref, lq2_ref, lk2_ref, lam_init)
    o = acc_ref[...] / l_ref[...]
    og0 = o[0:blk] - lam * o[blk:2 * blk]
    og1 = o[2 * blk:3 * blk] - lam * o[3 * blk:4 * blk]
    o_ref[0] = jnp.concatenate([og0, og1], axis=1)


def _attn_a_prompt(q, kt, vb, lams, lam_init):
    b, t, _ = q.shape
    blk = min(ATTN_BLOCK, t)
    lam_spec = pl.BlockSpec((1, HEAD_DIM), lambda bi, h, i: (0, 0))
    qw = G_A * 2 * HEAD_DIM
    return pl.pallas_call(
        functools.partial(_attn_a_kernel, blk=blk, lam_init=lam_init),
        grid=(b, KVH_A, t // blk),
        in_specs=[lam_spec] * 4 + [
            pl.BlockSpec((1, blk, qw), lambda bi, h, i: (bi, i, h)),
            pl.BlockSpec((1, 2 * HEAD_DIM, t), lambda bi, h, i: (bi, h, 0)),
            pl.BlockSpec((1, t, VD_A), lambda bi, h, i: (bi, 0, h))],
        out_specs=pl.BlockSpec((1, blk, G_A * VD_A), lambda bi, h, i: (bi, i, h)),
        out_shape=jax.ShapeDtypeStruct((b, t, KVH_A * G_A * VD_A), F32),
        scratch_shapes=[pltpu.VMEM((4 * blk, 1), F32), pltpu.VMEM((4 * blk, 1), F32),
                        pltpu.VMEM((4 * blk, VD_A), F32)],
        compiler_params=_cparams(("parallel", "parallel", "arbitrary")),
        name="attn_a_prompt",
    )(*lams, q, kt, vb)


def _attn_a_sample_kernel(pt_ref, lq1_ref, lk1_ref, lq2_ref, lk2_ref, q_ref, kn_ref, vn_ref, *rest,
                          pps, dec_t, lam_init):
    k_refs, v_refs = rest[:pps], rest[pps:2 * pps]
    o_ref, m_ref, l_ref, acc_ref = rest[2 * pps:]
    s_id = pl.program_id(1)
    nt = (((1,), (1,)), ((), ()))
    q = q_ref[0]

    @pl.when(s_id == 0)
    def _():
        m_ref[...] = jnp.full(m_ref.shape, NEG_INF, F32)
        l_ref[...] = jnp.zeros(l_ref.shape, F32)
        acc_ref[...] = jnp.zeros(acc_ref.shape, F32)

    s = jnp.concatenate([lax.dot_general(q, kr[...].astype(BF16), nt, preferred_element_type=F32)
                         for kr in k_refs], axis=1)
    v = jnp.concatenate([vr[...].astype(BF16) for vr in v_refs], axis=0)
    _softmax_step(s, v, m_ref, l_ref, acc_ref)

    @pl.when(s_id == pl.num_programs(1) - 1)
    def _():
        rows = q.shape[0]
        pad = jnp.zeros((PAGE_SIZE - dec_t, kn_ref.shape[2]), F32)
        kn = jnp.concatenate([kn_ref[0], pad], axis=0).astype(BF16)
        vn = jnp.concatenate([vn_ref[0], pad], axis=0).astype(BF16)
        sn = lax.dot_general(q, kn, nt, preferred_element_type=F32)
        tq = lax.broadcasted_iota(I32, (rows // dec_t, dec_t, PAGE_SIZE), 1).reshape(rows, PAGE_SIZE)
        tk = lax.broadcasted_iota(I32, (rows, PAGE_SIZE), 1)
        sn = jnp.where(tk <= tq, sn, NEG_INF)
        _softmax_step(sn, vn, m_ref, l_ref, acc_ref)
        lam = _diff_lambda(lq1_ref, lk1_ref, lq2_ref, lk2_ref, lam_init)
        o = acc_ref[...] / l_ref[...]
        outs = []
        for h in range(KVH_A):
            for g in range(G_A):
                r0 = ((h * G_A + g) * 2) * dec_t
                blk0 = o[r0:r0 + dec_t, h * VD_A:(h + 1) * VD_A]
                blk1 = o[r0 + dec_t:r0 + 2 * dec_t, h * VD_A:(h + 1) * VD_A]
                outs.append(blk0 - lam * blk1)
        o_ref[0] = jnp.concatenate(outs, axis=1)


def _attn_a_sample(q, k_new, v_new, cache_k, cache_v, page_table, lams, lam_init):
    db, dec_t, _ = q.shape
    n_pages = page_table.shape[1]
    pps = PAGES_PER_STEP
    kw = k_new.shape[2]
    q6 = q.reshape(db, dec_t, KVH_A, G_A, 2, HEAD_DIM)
    eye_h = jnp.eye(KVH_A, dtype=q.dtype)
    eye_m = jnp.eye(2, dtype=q.dtype)
    qbd = jnp.einsum('btkgmd,kK,mM->bkgmtKMd', q6, eye_h, eye_m).reshape(db, KVH_A * G_A * 2 * dec_t, kw)
    rows = qbd.shape[1]
    lam_spec = pl.BlockSpec((1, HEAD_DIM), lambda b, s, pt: (0, 0))

    def page_spec(i):
        return pl.BlockSpec((None, PAGE_SIZE, kw), lambda b, s, pt, i=i: (pt[b * n_pages + s * pps + i], 0, 0))

    grid_spec = pltpu.PrefetchScalarGridSpec(
        num_scalar_prefetch=1,
        grid=(db, n_pages // pps),
        in_specs=[lam_spec] * 4 + [
            pl.BlockSpec((1, rows, kw), lambda b, s, pt: (b, 0, 0)),
            pl.BlockSpec((1, dec_t, kw), lambda b, s, pt: (b, 0, 0)),
            pl.BlockSpec((1, dec_t, kw), lambda b, s, pt: (b, 0, 0))]
        + [page_spec(i) for i in range(pps)] + [page_spec(i) for i in range(pps)],
        out_specs=pl.BlockSpec((1, dec_t, KVH_A * G_A * VD_A), lambda b, s, pt: (b, 0, 0)),
        scratch_shapes=[pltpu.VMEM((rows, 1), F32), pltpu.VMEM((rows, 1), F32), pltpu.VMEM((rows, kw), F32)],
    )
    return pl.pallas_call(
        functools.partial(_attn_a_sample_kernel, pps=pps, dec_t=dec_t, lam_init=lam_init),
        grid_spec=grid_spec,
        out_shape=jax.ShapeDtypeStruct((db, dec_t, KVH_A * G_A * VD_A), F32),
        compiler_params=_cparams(("parallel", "arbitrary")),
        name="attn_a_sample",
    )(page_table.reshape(-1), *lams, qbd, k_new, v_new, *([cache_k] * pps), *([cache_v] * pps))


def _a_out_kernel(o_ref, h_ref, gs_ref, w_ref, out_ref, x_ref, *, post_scale):
    gs = gs_ref[...]
    for hd in range(o_ref.shape[1] // VD_A):
        oh = o_ref[:, hd * VD_A:(hd + 1) * VD_A]
        x_ref[:, hd * VD_A:(hd + 1) * VD_A] = (_rms(oh, gs) * post_scale).astype(BF16)
    out_ref[...] = h_ref[...] + jnp.dot(x_ref[...], w_ref[...], preferred_element_type=F32)


def _a_out(o, h, g_sub, w, lam_init):
    n, d = h.shape
    tm = _row_tile(n)
    row = lambda i: (i, 0)
    const = lambda i: (0, 0)
    return pl.pallas_call(
        functools.partial(_a_out_kernel, post_scale=1.0 - lam_init),
        grid=(n // tm,),
        in_specs=[pl.BlockSpec((tm, o.shape[1]), row), pl.BlockSpec((tm, d), row),
                  pl.BlockSpec((1, VD_A), const), pl.BlockSpec(w.shape, const)],
        out_specs=pl.BlockSpec((tm, d), row),
        out_shape=jax.ShapeDtypeStruct((n, d), F32),
        scratch_shapes=[pltpu.VMEM((tm, o.shape[1]), BF16)],
        compiler_params=_cparams(("parallel",)),
        name="a_out",
    )(o, h, g_sub.reshape(1, VD_A), w)


def _b_proj_kernel(h_ref, g_ref, w_ref, c_ref, sa_ref, sb_ref,
                   q_ref, k_ref, v_ref, kt_ref, vb_ref, qi_ref, kw_ref, kit_ref, *, wi_scale):
    x = _rms(h_ref[...], g_ref[...]).astype(BF16)
    y = jnp.dot(x, w_ref[...], preferred_element_type=F32)
    c, sa, sb = c_ref[...], sa_ref[...], sb_ref[...]
    o1 = B_Q_W
    o2 = o1 + B_K_W
    o3 = o2 + B_V_W
    o4 = o3 + B_QI_W
    q = _rope(y[:, :o1], c, sa, sb)
    k = _rope(y[:, o1:o2], c, sa, sb)
    v = y[:, o2:o3]
    qi = _rope(y[:, o3:o4], c, sa, sb)
    tail = y[:, o4:o4 + LANES]
    lane = lax.broadcasted_iota(I32, tail.shape, 1)
    kw = jnp.where(lane < HEAD_DIM, _rope(tail, c, sa, sb), tail * wi_scale)
    q_ref[...] = (q * (HEAD_DIM ** -0.5)).astype(BF16)
    k_ref[...] = k
    v_ref[...] = v
    kt_ref[...] = k.T.astype(BF16)
    vb_ref[...] = v.astype(BF16)
    qi_ref[...] = qi.astype(BF16)
    kw_ref[...] = kw
    kit_ref[...] = kw.T[:HEAD_DIM, :].astype(BF16)


def _b_proj(h, g, w_pad, tables, nb):
    n, d = h.shape
    t = n // nb
    tm = _row_tile(t)
    nt = t // tm
    const = lambda i: (0, 0)
    row = lambda i: (i, 0)
    tab = lambda i: (i % nt, 0)
    tr = lambda i: (i // nt, 0, i % nt)
    wi_scale = (IDX_H ** -0.5) * (HEAD_DIM ** -0.5)
    return pl.pallas_call(
        functools.partial(_b_proj_kernel, wi_scale=wi_scale),
        grid=(n // tm,),
        in_specs=[pl.BlockSpec((tm, d), row), pl.BlockSpec((1, d), const), pl.BlockSpec(w_pad.shape, const),
                  pl.BlockSpec((tm, LANES), tab), pl.BlockSpec((tm, LANES), tab), pl.BlockSpec((tm, LANES), tab)],
        out_specs=[pl.BlockSpec((tm, B_Q_W), row), pl.BlockSpec((tm, B_K_W), row), pl.BlockSpec((tm, B_V_W), row),
                   pl.BlockSpec((None, B_K_W, tm), tr), pl.BlockSpec((tm, B_V_W), row),
                   pl.BlockSpec((tm, B_QI_W), row), pl.BlockSpec((tm, LANES), row),
                   pl.BlockSpec((None, HEAD_DIM, tm), tr)],
        out_shape=[jax.ShapeDtypeStruct((n, B_Q_W), BF16), jax.ShapeDtypeStruct((n, B_K_W), F32),
                   jax.ShapeDtypeStruct((n, B_V_W), F32), jax.ShapeDtypeStruct((nb, B_K_W, t), BF16),
                   jax.ShapeDtypeStruct((n, B_V_W), BF16), jax.ShapeDtypeStruct((n, B_QI_W), BF16),
                   jax.ShapeDtypeStruct((n, LANES), F32), jax.ShapeDtypeStruct((nb, HEAD_DIM, t), BF16)],
        compiler_params=_cparams(("parallel",)),
        name="b_proj",
    )(h, g.reshape(1, d), w_pad, *tables)


def _score_keys(score):
    bits = lax.bitcast_convert_type(score + 0.0, I32)
    return bits ^ ((bits >> 31) & 0x7FFFFFFF)


def _count(pred):
    w = pred.shape[1]
    ones = jnp.where(pred, 1.0, 0.0)
    part = ones[:, :LANES]
    for j in range(1, w // LANES):
        part = part + ones[:, j * LANES:(j + 1) * LANES]
    return jnp.sum(part, axis=1, keepdims=True)


def _kth_largest(count_ge, rows, k):
    def body(b, t):
        cand = t + (jnp.int32(1) << (31 - b))
        return jnp.where(count_ge(cand) >= k, cand, t)
    return lax.fori_loop(0, 32, body, jnp.full((rows, 1), INT_MIN, I32))


def _tie_cutoff(count_eq_below, rows, need, n_bits):
    def body(b, c):
        cand = c + (jnp.int32(1) << (n_bits - 1 - b))
        return jnp.where(count_eq_below(cand) <= need, cand, c)
    return lax.fori_loop(0, n_bits, body, jnp.zeros((rows, 1), I32))


def _dsa_prompt_kernel(qi_ref, kw_ref, kit_ref, q_ref, kt_ref, v_ref, o_ref,
                       keys_ref, m_ref, l_ref, acc_ref, *, blk, n_sel, idx_bits):
    i = pl.program_id(1)
    n_chunks = i + 1
    qi = qi_ref[0]
    kwv = kw_ref[0]
    w_cols = [kwv[:, HEAD_DIM + h:HEAD_DIM + h + 1] for h in range(IDX_H)]
    qi_heads = [qi[:, h * HEAD_DIM:(h + 1) * HEAD_DIM] for h in range(IDX_H)]
    qpos = lax.broadcasted_iota(I32, (blk, blk), 0)
    kcol = lax.broadcasted_iota(I32, (blk, blk), 1)

    def score_body(j, carry):
        start = pl.multiple_of(j * blk, blk)
        kit = kit_ref[0, :, pl.ds(start, blk)]
        score = jnp.zeros((blk, blk), F32)
        for h in range(IDX_H):
            sc = jnp.dot(qi_heads[h], kit, preferred_element_type=F32)
            score = score + jnp.maximum(sc, 0.0) * w_cols[h]
        keys = _score_keys(score)
        keys = jnp.where(kcol <= qpos + (i - j) * blk, keys, INT_MIN)
        keys_ref[:, pl.ds(start, blk)] = keys
        return carry

    lax.fori_loop(0, n_chunks, score_body, 0)

    def chunk_count(pred_fn):
        def body(j, cnt):
            start = pl.multiple_of(j * blk, blk)
            return cnt + _count(pred_fn(keys_ref[:, pl.ds(start, blk)], j * blk + kcol))
        return lax.fori_loop(0, n_chunks, body, jnp.zeros((blk, 1), F32))

    k = float(n_sel)
    thr = _kth_largest(lambda cand: chunk_count(lambda ky, ix: ky >= cand), blk, k)
    need = k - chunk_count(lambda ky, ix: ky > thr)
    cut = _tie_cutoff(lambda c: chunk_count(lambda ky, ix: (ky == thr) & (ix < c)), blk, need, idx_bits)

    def bias_body(j, carry):
        start = pl.multiple_of(j * blk, blk)
        ky = keys_ref[:, pl.ds(start, blk)]
        sel = (ky > thr) | ((ky == thr) & (j * blk + kcol < cut))
        sel = sel & (kcol <= qpos + (i - j) * blk)
        bias = jnp.where(sel, 0.0, NEG_INF)
        keys_ref[:, pl.ds(start, blk)] = lax.bitcast_convert_type(bias, I32)
        return carry

    lax.fori_loop(0, n_chunks, bias_body, 0)

    m_ref[...] = jnp.full(m_ref.shape, NEG_INF, F32)
    l_ref[...] = jnp.zeros(l_ref.shape, F32)
    acc_ref[...] = jnp.zeros(acc_ref.shape, F32)
    q = q_ref[0]
    qs = [jnp.concatenate([q[:, (kv * G_B + g) * HEAD_DIM:(kv * G_B + g + 1) * HEAD_DIM] for g in range(G_B)],
                          axis=0) for kv in range(KVH_B)]

    def attn_body(j, carry):
        start = pl.multiple_of(j * blk, blk)
        bias = lax.bitcast_convert_type(keys_ref[:, pl.ds(start, blk)], F32)
        for kv in range(KVH_B):
            kt = kt_ref[0, kv * HEAD_DIM:(kv + 1) * HEAD_DIM, pl.ds(start, blk)]
            s = jnp.dot(qs[kv], kt, preferred_element_type=F32)
            s = (s.reshape(G_B, blk, blk) + bias[None]).reshape(G_B * blk, blk)
            v = v_ref[0, pl.ds(start, blk), kv * HEAD_DIM:(kv + 1) * HEAD_DIM]
            _softmax_step(s, v, m_ref.at[kv], l_ref.at[kv], acc_ref.at[kv])
        return carry

    lax.fori_loop(0, n_chunks, attn_body, 0)

    for kv in range(KVH_B):
        o = acc_ref[kv] / l_ref[kv]
        for g in range(G_B):
            hd = kv * G_B + g
            o_ref[0, :, hd * HEAD_DIM:(hd + 1) * HEAD_DIM] = o[g * blk:(g + 1) * blk]


def _dsa_prompt(qi, kw, kit, q, kt, vb):
    b, t, _ = q.shape
    blk = min(ATTN_BLOCK, t)
    n_sel = min(TOPK_MAX, t // 4)
    idx_bits = int(t).bit_length()
    rows = G_B * blk
    return pl.pallas_call(
        functools.partial(_dsa_prompt_kernel, blk=blk, n_sel=n_sel, idx_bits=idx_bits),
        grid=(b, t // blk),
        in_specs=[pl.BlockSpec((1, blk, B_QI_W), lambda bi, i: (bi, i, 0)),
                  pl.BlockSpec((1, blk, LANES), lambda bi, i: (bi, i, 0)),
                  pl.BlockSpec((1, HEAD_DIM, t), lambda bi, i: (bi, 0, 0)),
                  pl.BlockSpec((1, blk, B_Q_W), lambda bi, i: (bi, i, 0)),
                  pl.BlockSpec((1, B_K_W, t), lambda bi, i: (bi, 0, 0)),
                  pl.BlockSpec((1, t, B_V_W), lambda bi, i: (bi, 0, 0))],
        out_specs=pl.BlockSpec((1, blk, B_Q_W), lambda bi, i: (bi, i, 0)),
        out_shape=jax.ShapeDtypeStruct((b, t, B_Q_W), F32),
        scratch_shapes=[pltpu.VMEM((blk, t), I32),
                        pltpu.VMEM((KVH_B, rows, 1), F32), pltpu.VMEM((KVH_B, rows, 1), F32),
                        pltpu.VMEM((KVH_B, rows, HEAD_DIM), F32)],
        compiler_params=_cparams(("parallel", "arbitrary")),
        name="dsa_prompt",
    )(qi, kw, kit, q, kt, vb)


def _dsa_sample_select_kernel(pt_ref, qi_ref, w_ref, kin_ref, *rest, pps, dec_t, n_sel, idx_bits):
    ki_refs = rest[:pps]
    bias_ref, biasn_ref, keys_ref, keyn_ref = rest[pps:]
    s_id = pl.program_id(1)
    nt = (((1,), (1,)), ((), ()))
    qi = qi_ref[0]
    w = w_ref[0]
    span = pps * PAGE_SIZE
    n_past = bias_ref.shape[2]

    def scores(ki):
        sc = lax.dot_general(qi, ki, nt, preferred_element_type=F32)
        sc = jnp.maximum(sc, 0.0) * w
        return jnp.sum(sc.reshape(IDX_H, dec_t, sc.shape[1]), axis=0)

    ki = jnp.concatenate([r[...].astype(BF16) for r in ki_refs], axis=0)
    keys_ref[:, pl.ds(pl.multiple_of(s_id * span, span), span)] = _score_keys(scores(ki))

    @pl.when(s_id == pl.num_programs(1) - 1)
    def _():
        pad = jnp.zeros((PAGE_SIZE - dec_t, HEAD_DIM), F32)
        kn = jnp.concatenate([kin_ref[0], pad], axis=0).astype(BF16)
        tq = lax.broadcasted_iota(I32, (dec_t, PAGE_SIZE), 0)
        tk = lax.broadcasted_iota(I32, (dec_t, PAGE_SIZE), 1)
        causal_new = tk <= tq
        keyn_ref[...] = jnp.where(causal_new, _score_keys(scores(kn)), INT_MIN)
        n_chunks = n_past // span
        lane = lax.broadcasted_iota(I32, (dec_t, span), 1)

        def load(j):
            start = pl.multiple_of(j * span, span)
            return keys_ref[:, pl.ds(start, span)]

        def total(pred_fn):
            def body(j, cnt):
                return cnt + _count(pred_fn(load(j), j * span + lane))
            cnt = lax.fori_loop(0, n_chunks, body, jnp.zeros((dec_t, 1), F32))
            return cnt + _count(pred_fn(keyn_ref[...], n_past + tk))

        k = float(n_sel)
        thr = _kth_largest(lambda cand: total(lambda ky, ix: ky >= cand), dec_t, k)
        need = k - total(lambda ky, ix: ky > thr)
        cut = _tie_cutoff(lambda c: total(lambda ky, ix: (ky == thr) & (ix < c)), dec_t, need, idx_bits)

        def select(ky, ix):
            return (ky > thr) | ((ky == thr) & (ix < cut))

        def bias_body(j, carry):
            start = pl.multiple_of(j * span, span)
            bias_ref[0, :, pl.ds(start, span)] = jnp.where(select(load(j), j * span + lane), 0.0, NEG_INF)
            return carry

        lax.fori_loop(0, n_chunks, bias_body, 0)
        biasn_ref[0] = jnp.where(select(keyn_ref[...], n_past + tk) & causal_new, 0.0, NEG_INF)


def _dsa_sample_select(qi, kw, cache_kidx, page_table):
    db, dec_t, _ = qi.shape
    n_pages = page_table.shape[1]
    pps = PAGES_PER_STEP
    past = n_pages * PAGE_SIZE
    n_sel = min(TOPK_MAX, (past + dec_t) // 4)
    idx_bits = int(past + PAGE_SIZE).bit_length()
    qi_ht = qi.reshape(db, dec_t, IDX_H, HEAD_DIM).transpose(0, 2, 1, 3).reshape(db, IDX_H * dec_t, HEAD_DIM)
    w_ht = kw[:, :, HEAD_DIM:HEAD_DIM + IDX_H].transpose(0, 2, 1).reshape(db, IDX_H * dec_t, 1)
    ki_new = kw[:, :, :HEAD_DIM]

    def page_spec(i):
        return pl.BlockSpec((None, PAGE_SIZE, HEAD_DIM), lambda b, s, pt, i=i: (pt[b * n_pages + s * pps + i], 0, 0))

    grid_spec = pltpu.PrefetchScalarGridSpec(
        num_scalar_prefetch=1,
        grid=(db, n_pages // pps),
        in_specs=[pl.BlockSpec((1, IDX_H * dec_t, HEAD_DIM), lambda b, s, pt: (b, 0, 0)),
                  pl.BlockSpec((1, IDX_H * dec_t, 1), lambda b, s, pt: (b, 0, 0)),
                  pl.BlockSpec((1, dec_t, HEAD_DIM), lambda b, s, pt: (b, 0, 0))]
        + [page_spec(i) for i in range(pps)],
        out_specs=[pl.BlockSpec((1, dec_t, past), lambda b, s, pt: (b, 0, 0)),
                   pl.BlockSpec((1, dec_t, PAGE_SIZE), lambda b, s, pt: (b, 0, 0))],
        scratch_shapes=[pltpu.VMEM((dec_t, past), I32), pltpu.VMEM((dec_t, PAGE_SIZE), I32)],
    )
    return pl.pallas_call(
        functools.partial(_dsa_sample_select_kernel, pps=pps, dec_t=dec_t, n_sel=n_sel, idx_bits=idx_bits),
        grid_spec=grid_spec,
        out_shape=[jax.ShapeDtypeStruct((db, dec_t, past), F32), jax.ShapeDtypeStruct((db, dec_t, PAGE_SIZE), F32)],
        compiler_params=_cparams(("parallel", "arbitrary")),
        name="dsa_sample_select",
    )(page_table.reshape(-1), qi_ht, w_ht, ki_new, *([cache_kidx] * pps))


def _dsa_sample_attn_kernel(pt_ref, q_ref, kn_ref, vn_ref, bias_ref, biasn_ref, *rest, pps, dec_t):
    k_refs, v_refs = rest[:pps], rest[pps:2 * pps]
    o_ref, m_ref, l_ref, acc_ref = rest[2 * pps:]
    s_id = pl.program_id(1)
    nt = (((1,), (1,)), ((), ()))
    q = q_ref[0]
    rows = q.shape[0]

    @pl.when(s_id == 0)
    def _():
        m_ref[...] = jnp.full(m_ref.shape, NEG_INF, F32)
        l_ref[...] = jnp.zeros(l_ref.shape, F32)
        acc_ref[...] = jnp.zeros(acc_ref.shape, F32)

    def add_bias(s, bias):
        n = s.shape[1]
        return (s.reshape(rows // dec_t, dec_t, n) + bias[None]).reshape(rows, n)

    s = jnp.concatenate([lax.dot_general(q, kr[...].astype(BF16), nt, preferred_element_type=F32)
                         for kr in k_refs], axis=1)
    v = jnp.concatenate([vr[...].astype(BF16) for vr in v_refs], axis=0)
    _softmax_step(add_bias(s, bias_ref[0]), v, m_ref, l_ref, acc_ref)

    @pl.when(s_id == pl.num_programs(1) - 1)
    def _():
        pad = jnp.zeros((PAGE_SIZE - dec_t, kn_ref.shape[2]), F32)
        kn = jnp.concatenate([kn_ref[0], pad], axis=0).astype(BF16)
        vn = jnp.concatenate([vn_ref[0], pad], axis=0).astype(BF16)
        sn = lax.dot_general(q, kn, nt, preferred_element_type=F32)
        _softmax_step(add_bias(sn, biasn_ref[0]), vn, m_ref, l_ref, acc_ref)
        o = acc_ref[...] / l_ref[...]
        outs = []
        for kv in range(KVH_B):
            for g in range(G_B):
                r0 = (kv * G_B + g) * dec_t
                outs.append(o[r0:r0 + dec_t, kv * HEAD_DIM:(kv + 1) * HEAD_DIM])
        o_ref[0] = jnp.concatenate(outs, axis=1)


def _dsa_sample_attn(q, k_new, v_new, bias, bias_new, cache_k, cache_v, page_table):
    db, dec_t, _ = q.shape
    n_pages = page_table.shape[1]
    pps = PAGES_PER_STEP
    kw = k_new.shape[2]
    q5 = q.reshape(db, dec_t, KVH_B, G_B, HEAD_DIM)
    qbd = jnp.einsum('btkgd,kK->bkgtKd', q5, jnp.eye(KVH_B, dtype=q.dtype)).reshape(db, KVH_B * G_B * dec_t, kw)
    rows = qbd.shape[1]
    span = pps * PAGE_SIZE

    def page_spec(i):
        return pl.BlockSpec((None, PAGE_SIZE, kw), lambda b, s, pt, i=i: (pt[b * n_pages + s * pps + i], 0, 0))

    grid_spec = pltpu.PrefetchScalarGridSpec(
        num_scalar_prefetch=1,
        grid=(db, n_pages // pps),
        in_specs=[pl.BlockSpec((1, rows, kw), lambda b, s, pt: (b, 0, 0)),
                  pl.BlockSpec((1, dec_t, kw), lambda b, s, pt: (b, 0, 0)),
                  pl.BlockSpec((1, dec_t, kw), lambda b, s, pt: (b, 0, 0)),
                  pl.BlockSpec((1, dec_t, span), lambda b, s, pt: (b, 0, s)),
                  pl.BlockSpec((1, dec_t, PAGE_SIZE), lambda b, s, pt: (b, 0, 0))]
        + [page_spec(i) for i in range(pps)] + [page_spec(i) for i in range(pps)],
        out_specs=pl.BlockSpec((1, dec_t, KVH_B * G_B * HEAD_DIM), lambda b, s, pt: (b, 0, 0)),
        scratch_shapes=[pltpu.VMEM((rows, 1), F32), pltpu.VMEM((rows, 1), F32), pltpu.VMEM((rows, kw), F32)],
    )
    return pl.pallas_call(
        functools.partial(_dsa_sample_attn_kernel, pps=pps, dec_t=dec_t),
        grid_spec=grid_spec,
        out_shape=jax.ShapeDtypeStruct((db, dec_t, KVH_B * G_B * HEAD_DIM), F32),
        compiler_params=_cparams(("parallel", "arbitrary")),
        name="dsa_sample_attn",
    )(page_table.reshape(-1), qbd, k_new, v_new, bias, bias_new, *([cache_k] * pps), *([cache_v] * pps))


def _proj_res_kernel(o_ref, h_ref, w_ref, out_ref):
    out_ref[...] = h_ref[...] + jnp.dot(o_ref[...].astype(BF16), w_ref[...], preferred_element_type=F32)


def _proj_res(o, h, w):
    n, d = h.shape
    tm = _row_tile(n)
    row = lambda i: (i, 0)
    return pl.pallas_call(
        _proj_res_kernel,
        grid=(n // tm,),
        in_specs=[pl.BlockSpec((tm, o.shape[1]), row), pl.BlockSpec((tm, d), row),
                  pl.BlockSpec(w.shape, lambda i: (0, 0))],
        out_specs=pl.BlockSpec((tm, d), row),
        out_shape=jax.ShapeDtypeStruct((n, d), F32),
        compiler_params=_cparams(("parallel",)),
        name="proj_res",
    )(o, h, w)


def kernel(x_prompt, x_sample, cache_a_k, cache_a_v, cache_b_k, cache_b_v, cache_b_kidx, page_table, p_prompt, p_sample, norm_ffn1, w_ffn1_in, w_ffn1_out, norm_mix, norm_ffn2, w_ffn2_in, w_ffn2_out, norm_ple, w_ple_gate, w_ple_proj, w_a_in, w_a_out, lambda_q1, lambda_k1, lambda_q2, lambda_k2, subln_a, w_b_in, w_b_out, norm_final):
    b, t, d = x_prompt.shape
    db, dec_t, _ = x_sample.shape
    depth = p_prompt.shape[0]
    n_pool = cache_a_k.shape[1]
    past = page_table.shape[1] * PAGE_SIZE
    bf = lambda w: w.astype(BF16)

    hp = x_prompt.reshape(b * t, d)
    hs = x_sample.reshape(db * dec_t, d)
    tab_p = _rope_tables(jnp.arange(t))
    tab_s = tuple(jnp.tile(a, (db, 1)) for a in _rope_tables(past + jnp.arange(dec_t)))

    a_kp, a_vp, a_ks, a_vs = [], [], [], []
    b_kp, b_vp, b_ip, b_ks, b_vs, b_is = [], [], [], [], [], []
    for i in range(depth):
        j = i // N_MIXERS
        w1i, w1o = bf(w_ffn1_in[i]), bf(w_ffn1_out[i])
        hp = _half_ffn(hp, norm_ffn1[i], w1i, w1o)
        hs = _half_ffn(hs, norm_ffn1[i], w1i, w1o)
        if i % N_MIXERS == 0:
            lam_init = 0.8 - 0.6 * float(np.exp(-0.3 * i))
            lams = tuple(a[j].reshape(1, HEAD_DIM) for a in (lambda_q1, lambda_k1, lambda_q2, lambda_k2))
            w_in, w_out = bf(w_a_in[j]), bf(w_a_out[j])
            qp, kp, vp, ktp, vbp = _a_proj(hp, norm_mix[i], w_in, tab_p, b)
            op = _attn_a_prompt(qp.reshape(b, t, -1), ktp, vbp.reshape(b, t, -1), lams, lam_init)
            hp = _a_out(op.reshape(b * t, -1), hp, subln_a[j], w_out, lam_init)
            qs, ks, vs, _, _ = _a_proj(hs, norm_mix[i], w_in, tab_s, 1)
            ck = cache_a_k[j].reshape(n_pool, PAGE_SIZE, -1)
            cv = cache_a_v[j].reshape(n_pool, PAGE_SIZE, -1)
            os_ = _attn_a_sample(qs.reshape(db, dec_t, -1), ks.reshape(db, dec_t, -1), vs.reshape(db, dec_t, -1),
                                 ck, cv, page_table, lams, lam_init)
            hs = _a_out(os_.reshape(db * dec_t, -1), hs, subln_a[j], w_out, lam_init)
            a_kp.append(kp.reshape(b, t, KVH_A, 2, HEAD_DIM))
            a_vp.append(vp.reshape(b, t, KVH_A, VD_A))
            a_ks.append(ks.reshape(db, dec_t, KVH_A, 2, HEAD_DIM))
            a_vs.append(vs.reshape(db, dec_t, KVH_A, VD_A))
        else:
            w_in = w_b_in[j]
            w_pad = bf(jnp.pad(w_in, ((0, 0), (0, B_Q_W + B_K_W + B_V_W + B_QI_W + LANES - w_in.shape[1]))))
            w_out = bf(w_b_out[j])
            qp, kp, vp, ktp, vbp, qip, kwp, kitp = _b_proj(hp, norm_mix[i], w_pad, tab_p, b)
            op = _dsa_prompt(qip.reshape(b, t, -1), kwp.reshape(b, t, -1), kitp, qp.reshape(b, t, -1), ktp,
                             vbp.reshape(b, t, -1))
            hp = _proj_res(op.reshape(b * t, -1), hp, w_out)
            qs, ks, vs, _, _, qis, kws, _ = _b_proj(hs, norm_mix[i], w_pad, tab_s, 1)
            kws3 = kws.reshape(db, dec_t, -1)
            ks3, vs3 = ks.reshape(db, dec_t, -1), vs.reshape(db, dec_t, -1)
            bias, bias_new = _dsa_sample_select(qis.reshape(db, dec_t, -1), kws3,
                                                cache_b_kidx[j], page_table)
            ck = cache_b_k[j].reshape(n_pool, PAGE_SIZE, -1)
            cv = cache_b_v[j].reshape(n_pool, PAGE_SIZE, -1)
            os_ = _dsa_sample_attn(qs.reshape(db, dec_t, -1), ks3, vs3, bias, bias_new, ck, cv, page_table)
            hs = _proj_res(os_.reshape(db * dec_t, -1), hs, w_out)
            b_kp.append(kp.reshape(b, t, KVH_B, HEAD_DIM))
            b_vp.append(vp.reshape(b, t, KVH_B, HEAD_DIM))
            b_ip.append(kwp[:, :HEAD_DIM].reshape(b, t, HEAD_DIM))
            b_ks.append(ks.reshape(db, dec_t, KVH_B, HEAD_DIM))
            b_vs.append(vs.reshape(db, dec_t, KVH_B, HEAD_DIM))
            b_is.append(kws[:, :HEAD_DIM].reshape(db, dec_t, HEAD_DIM))
        w2i, w2o = bf(w_ffn2_in[i]), bf(w_ffn2_out[i])
        hp = _half_ffn(hp, norm_ffn2[i], w2i, w2o)
        hs = _half_ffn(hs, norm_ffn2[i], w2i, w2o)
        g_final = norm_final if i == depth - 1 else None
        wg, wp = bf(w_ple_gate[i]), bf(w_ple_proj[i])
        hp = _ple(hp, p_prompt[i].reshape(b * t, -1), norm_ple[i], wg, wp, g_final)
        hs = _ple(hs, p_sample[i].reshape(db * dec_t, -1), norm_ple[i], wg, wp, g_final)
    return (hp.reshape(b, t, d), hs.reshape(db, dec_t, d),
            jnp.stack(a_kp), jnp.stack(a_vp), jnp.stack(b_kp), jnp.stack(b_vp), jnp.stack(b_ip),
            jnp.stack(a_ks), jnp.stack(a_vs), jnp.stack(b_ks), jnp.stack(b_vs), jnp.stack(b_is))
```
